```python
import jax
import jax.numpy as jnp
from jax import lax
import numpy as np

D_MODEL = 1024
BATCH = 4
SEQ = 8192
DEPTH = 2

CTX_LEN = 256
GRID_W = 64
MIX_WIDTH = D_MODEL
ATTN_HEAD_DIM = 64
ATTN_HEADS = MIX_WIDTH // (2 * ATTN_HEAD_DIM)
ATTN_KV_HEADS = ATTN_HEADS // 4
ATTN_GROUP = ATTN_HEADS // ATTN_KV_HEADS
ATTN_WIDTH = ATTN_HEADS * ATTN_HEAD_DIM
KV_WIDTH = ATTN_KV_HEADS * ATTN_HEAD_DIM
WINDOW = 128
BLOCK = 128
ROPE_BASE = 10000.0
RET_HEADS = 4
RET_WIDTH = MIX_WIDTH - ATTN_WIDTH
RET_V_DIM = RET_WIDTH // RET_HEADS
RET_QK_DIM = RET_V_DIM // 2
RET_QK_WIDTH = RET_HEADS * RET_QK_DIM
CHUNK = 128
IN_WIDTH = ATTN_WIDTH + 2 * KV_WIDTH + 2 * RET_QK_WIDTH + 2 * RET_WIDTH
D_FF = 256 * ((8 * D_MODEL // 3 + 255) // 256)
FFN_RESIDUAL = 0.5
N_MOD = 9
RMS_EPS = 1e-6
GN_EPS = 1e-5
NEG_INF = -1e30

kernel_name = 'hybrid_swa_retention_macaron_dit'


def rmsnorm(x, g):
    xf = x.astype(jnp.float32)
    y = xf * lax.rsqrt(jnp.mean(xf * xf, axis=-1, keepdims=True) + RMS_EPS)
    return (y * g.astype(jnp.float32)).astype(x.dtype)


def rotate(x, ang):
    f = ang.shape[-1]
    cs = jnp.cos(ang)[:, None, :].astype(x.dtype)
    sn = jnp.sin(ang)[:, None, :].astype(x.dtype)
    x1, x2 = x[..., :f], x[..., f:]
    return jnp.concatenate([x1 * cs - x2 * sn, x2 * cs + x1 * sn], axis=-1)


def axial_rope_angles(n_tokens):
    rows = n_tokens // GRID_W
    row = jnp.repeat(jnp.arange(rows, dtype=jnp.float32), GRID_W)
    col = jnp.tile(jnp.arange(GRID_W, dtype=jnp.float32), rows)
    nf = ATTN_HEAD_DIM // 4
    inv = ROPE_BASE ** (-jnp.arange(nf, dtype=jnp.float32) / nf)
    return row[:, None] * inv, col[:, None] * inv


def apply_axial(x, ang_row, ang_col):
    half = ATTN_HEAD_DIM // 2
    return jnp.concatenate([rotate(x[..., :half], ang_row), rotate(x[..., half:], ang_col)], axis=-1)


def retention_angles(n_tokens):
    t = jnp.arange(n_tokens, dtype=jnp.float32)
    inv = ROPE_BASE ** (-jnp.linspace(0.0, 1.0, RET_QK_DIM // 2, dtype=jnp.float32))
    return t[:, None] * inv


def swiglu_sublayer(h, shift, scale, gate, g_pre, g_post, wi, wo):
    u = rmsnorm(h, g_pre) * (1 + scale) + shift
    a, b = jnp.split(u @ wi, 2, axis=-1)
    y = (jax.nn.silu(a) * b) @ wo
    return h + FFN_RESIDUAL * gate * rmsnorm(y, g_post)


def project(u, w):
    b, t = u.shape[:2]
    sizes = [ATTN_WIDTH, KV_WIDTH, KV_WIDTH, RET_QK_WIDTH, RET_QK_WIDTH, RET_WIDTH, RET_WIDTH]
    qa, ka, va, qr, kr, vr, gr = jnp.split(u @ w, [int(o) for o in np.cumsum(sizes)[:-1]], axis=-1)
    qa = qa.reshape(b, t, ATTN_HEADS, ATTN_HEAD_DIM)
    ka = ka.reshape(b, t, ATTN_KV_HEADS, ATTN_HEAD_DIM)
    va = va.reshape(b, t, ATTN_KV_HEADS, ATTN_HEAD_DIM)
    qr = qr.reshape(b, t, RET_HEADS, RET_QK_DIM)
    kr = kr.reshape(b, t, RET_HEADS, RET_QK_DIM) * (RET_QK_DIM ** -0.5)
    vr = vr.reshape(b, t, RET_HEADS, RET_V_DIM)
    return qa, ka, va, qr, kr, vr, gr


def window_attention(q, k, v, k_ctx, v_ctx, sink):
    b, s = q.shape[:2]
    nb = s // BLOCK
    qb = q.reshape(b, nb, BLOCK, ATTN_KV_HEADS, ATTN_GROUP, ATTN_HEAD_DIM)

    def band(t):
        tp = jnp.pad(t, ((0, 0), (BLOCK, BLOCK), (0, 0), (0, 0)))
        tp = tp.reshape(b, nb + 2, BLOCK, ATTN_KV_HEADS, ATTN_HEAD_DIM)
        return jnp.concatenate([tp[:, :-2], tp[:, 1:-1], tp[:, 2:]], axis=2)

    kb, vb = band(k), band(v)
    qpos = jnp.arange(nb)[:, None] * BLOCK + jnp.arange(BLOCK)[None, :]
    kpos = (jnp.arange(nb)[:, None] - 1) * BLOCK + jnp.arange(3 * BLOCK)[None, :]
    rel = kpos[:, None, :] - qpos[:, :, None]
    valid = (jnp.abs(rel) <= WINDOW) & (kpos[:, None, :] >= 0) & (kpos[:, None, :] < s)
    scale = ATTN_HEAD_DIM ** -0.5
    s_loc = jnp.einsum('bnqkgd,bnskd->bnkgqs', qb, kb).astype(jnp.float32) * scale
    s_loc = jnp.where(valid[None, :, None, None], s_loc, NEG_INF)
    s_ctx = jnp.einsum('bnqkgd,blkd->bnkgql', qb, k_ctx).astype(jnp.float32) * scale
    sk = sink.astype(jnp.float32).reshape(ATTN_KV_HEADS, ATTN_GROUP)[None, None, :, :, None, None]
    m = jnp.maximum(jnp.maximum(s_loc.max(-1, keepdims=True), s_ctx.max(-1, keepdims=True)), sk)
    e_loc = jnp.exp(s_loc - m)
    e_ctx = jnp.exp(s_ctx - m)
    denom = e_loc.sum(-1, keepdims=True) + e_ctx.sum(-1, keepdims=True) + jnp.exp(sk - m)
    out = (jnp.einsum('bnkgqs,bnskd->bnqkgd', (e_loc / denom).astype(v.dtype), vb)
           + jnp.einsum('bnkgql,blkd->bnqkgd', (e_ctx / denom).astype(v.dtype), v_ctx))
    return out.reshape(b, s, ATTN_WIDTH)


def context_attention(q, k, v, sink):
    b, l = q.shape[:2]
    qg = q.reshape(b, l, ATTN_KV_HEADS, ATTN_GROUP, ATTN_HEAD_DIM)
    sc = jnp.einsum('blkgd,bmkd->bkglm', qg, k).astype(jnp.float32) * (ATTN_HEAD_DIM ** -0.5)
    sk = sink.astype(jnp.float32).reshape(ATTN_KV_HEADS, ATTN_GROUP)[None, :, :, None, None]
    m = jnp.maximum(sc.max(-1, keepdims=True), sk)
    e = jnp.exp(sc - m)
    p = e / (e.sum(-1, keepdims=True) + jnp.exp(sk - m))
    out = jnp.einsum('bkglm,bmkd->blkgd', p.astype(v.dtype), v)
    return out.reshape(b, l, ATTN_WIDTH)


def decay_tables(log_g, strict):
    idx = jnp.arange(CHUNK, dtype=jnp.float32)
    diff = idx[:, None] - idx[None, :]
    mask = diff > 0 if strict else diff >= 0
    intra = jnp.where(mask[None], jnp.exp(jnp.maximum(diff, 0.0)[None] * log_g[:, None, None]), 0.0)
    xi = jnp.exp((idx + 1.0)[None, :] * log_g[:, None])
    zeta = jnp.exp((CHUNK - 1.0 - idx)[None, :] * log_g[:, None])
    return intra, xi, zeta, jnp.exp(CHUNK * log_g)


def retention_dir(q, k, v, log_g, s0, strict):
    b, t = q.shape[:2]
    n = t // CHUNK
    qc = q.reshape(b, n, CHUNK, RET_HEADS, RET_QK_DIM)
    kc = k.reshape(b, n, CHUNK, RET_HEADS, RET_QK_DIM)
    vc = v.reshape(b, n, CHUNK, RET_HEADS, RET_V_DIM)
    intra, xi, zeta, chunk_decay = decay_tables(log_g, strict)
    scores = jnp.einsum('bnqhd,bnshd->bnhqs', qc, kc) * intra
    out_inner = jnp.einsum('bnhqs,bnshe->bnqhe', scores, vc)
    kv = jnp.einsum('bnshd,hs,bnshe->nbhde', kc, zeta, vc)

    def step(state, kv_i):
        return chunk_decay[None, :, None, None] * state + kv_i, state

    _, s_prev = lax.scan(step, s0, kv)
    out_cross = jnp.einsum('bnqhd,hq,nbhde->bnqhe', qc, xi, s_prev)
    return (out_inner + out_cross).reshape(b, t, RET_HEADS, RET_V_DIM)


def bidirectional_retention(q, k, v, log_f, log_b, s_f, s_b):
    flip = lambda a: a[:, ::-1]
    out_f = retention_dir(q, k, v, log_f, s_f, False)
    out_b = flip(retention_dir(flip(q), flip(k), flip(v), log_b, s_b, True))
    return out_f + out_b


def context_state(k, v, log_g, reverse):
    l = k.shape[1]
    idx = jnp.arange(l, dtype=jnp.float32)
    w = jnp.exp((idx if reverse else (l - 1.0 - idx))[None, :] * log_g[:, None])
    return jnp.einsum('blhd,hl,blhe->bhde', k, w, v)


def retention_output(y, g, gn_gain):
    b, t = y.shape[:2]
    yf = y.astype(jnp.float32)
    mu = yf.mean(-1, keepdims=True)
    var = ((yf - mu) ** 2).mean(-1, keepdims=True)
    yn = (yf - mu) * lax.rsqrt(var + GN_EPS) * gn_gain.astype(jnp.float32).reshape(RET_HEADS, RET_V_DIM)
    return (jax.nn.silu(g.astype(jnp.float32)) * yn.reshape(b, t, RET_WIDTH)).astype(g.dtype)


def setup_inputs(seed: int = 0) -> dict:
    key = jax.random.key(seed)
    ks = jax.random.split(key, 18)
    nrm = jax.random.normal
    f32 = jnp.float32
    base_logit = jnp.log(2.0 ** (5.0 + jnp.arange(RET_HEADS, dtype=f32)) - 1.0)
    return {
        'x': nrm(ks[0], (BATCH, SEQ, D_MODEL), f32),
        'c': nrm(ks[1], (BATCH, D_MODEL), f32),
        'ctx': nrm(ks[2], (BATCH, CTX_LEN, D_MODEL), f32),
        'c_ctx': nrm(ks[3], (D_MODEL,), f32),
        'ada_w': nrm(ks[4], (DEPTH, D_MODEL, N_MOD * D_MODEL), f32) * (0.5 * D_MODEL ** -0.5),
        'ada_b': nrm(ks[5], (DEPTH, N_MOD * D_MODEL), f32) * 0.01,
        'norm_pre': 1.0 + 0.05 * nrm(ks[6], (DEPTH, 3, D_MODEL), f32),
        'norm_post': 1.0 + 0.05 * nrm(ks[7], (DEPTH, 3, D_MODEL), f32),
        'ffn1_wi': nrm(ks[8], (DEPTH, D_MODEL, 2 * D_FF), f32) * D_MODEL ** -0.5,
        'ffn1_wo': nrm(ks[9], (DEPTH, D_FF, D_MODEL), f32) * D_FF ** -0.5,
        'ffn2_wi': nrm(ks[10], (DEPTH, D_MODEL, 2 * D_FF), f32) * D_MODEL ** -0.5,
        'ffn2_wo': nrm(ks[11], (DEPTH, D_FF, D_MODEL), f32) * D_FF ** -0.5,
        'w_in': nrm(ks[12], (DEPTH, D_MODEL, IN_WIDTH), f32) * D_MODEL ** -0.5,
        'w_out': nrm(ks[13], (DEPTH, MIX_WIDTH, D_MODEL), f32) * MIX_WIDTH ** -0.5,
        'attn_sink': 0.5 * nrm(ks[14], (DEPTH, ATTN_HEADS), f32),
        'ret_decay_fwd': base_logit + 0.05 * nrm(ks[15], (DEPTH, RET_HEADS), f32),
        'ret_decay_bwd': base_logit + 0.05 * nrm(ks[16], (DEPTH, RET_HEADS), f32),
        'ret_gn': 1.0 + 0.05 * nrm(ks[17], (DEPTH, RET_WIDTH), f32),
    }


def reference(x, c, ctx, c_ctx, ada_w, ada_b, norm_pre, norm_post, ffn1_wi, ffn1_wo, ffn2_wi, ffn2_wo,
              w_in, w_out, attn_sink, ret_decay_fwd, ret_decay_bwd, ret_gn):
    b, s = x.shape[:2]
    ang_row, ang_col = axial_rope_angles(s)
    ang_ret = retention_angles(s)
    h, hc = x, ctx
    for l in range(DEPTH):
        last = l == DEPTH - 1
        mod = (jax.nn.silu(c) @ ada_w[l] + ada_b[l]).reshape(b, N_MOD, 1, D_MODEL)
        mod_c = (jax.nn.silu(c_ctx) @ ada_w[l] + ada_b[l]).reshape(N_MOD, 1, D_MODEL)

        h = swiglu_sublayer(h, mod[:, 0], mod[:, 1], mod[:, 2], norm_pre[l, 0], norm_post[l, 0], ffn1_wi[l], ffn1_wo[l])
        hc = swiglu_sublayer(hc, mod_c[0], mod_c[1], mod_c[2], norm_pre[l, 0], norm_post[l, 0], ffn1_wi[l], ffn1_wo[l])

        u = rmsnorm(h, norm_pre[l, 1]) * (1 + mod[:, 4]) + mod[:, 3]
        uc = rmsnorm(hc, norm_pre[l, 1]) * (1 + mod_c[4]) + mod_c[3]
        qa, ka, va, qr, kr, vr, gr = project(u, w_in[l])
        qac, kac, vac, qrc, krc, vrc, grc = project(uc, w_in[l])
        log_f = jax.nn.log_sigmoid(ret_decay_fwd[l].astype(jnp.float32))
        log_b = jax.nn.log_sigmoid(ret_decay_bwd[l].astype(jnp.float32))
        s_f = context_state(krc, vrc, log_f, reverse=False)
        s_b = context_state(krc, vrc, log_b, reverse=True)

        attn = window_attention(apply_axial(qa, ang_row, ang_col), apply_axial(ka, ang_row, ang_col), va,
                                kac, vac, attn_sink[l])
        ret = retention_output(bidirectional_retention(rotate(qr, ang_ret), rotate(kr, ang_ret), vr,
                                                       log_f, log_b, s_f, s_b), gr, ret_gn[l])
        y = jnp.concatenate([attn, ret], axis=-1) @ w_out[l]
        h = h + mod[:, 5] * rmsnorm(y, norm_post[l, 1])

        if not last:
            zero_state = jnp.zeros((hc.shape[0], RET_HEADS, RET_QK_DIM, RET_V_DIM), jnp.float32)
            attn_c = context_attention(qac, kac, vac, attn_sink[l])
            ret_c = retention_output(bidirectional_retention(qrc, krc, vrc, log_f, log_b, zero_state, zero_state),
                                     grc, ret_gn[l])
            yc = jnp.concatenate([attn_c, ret_c], axis=-1) @ w_out[l]
            hc = hc + mod_c[5] * rmsnorm(yc, norm_post[l, 1])
            hc = swiglu_sublayer(hc, mod_c[6], mod_c[7], mod_c[8], norm_pre[l, 2], norm_post[l, 2], ffn2_wi[l], ffn2_wo[l])

        h = swiglu_sublayer(h, mod[:, 6], mod[:, 7], mod[:, 8], norm_pre[l, 2], norm_post[l, 2], ffn2_wi[l], ffn2_wo[l])
    return h
```

```python
import functools

import jax
import jax.numpy as jnp
from jax import lax
from jax.experimental import pallas as pl
from jax.experimental.pallas import tpu as pltpu

F32 = jnp.float32
BF16 = jnp.bfloat16

D_MODEL = 1024
DEPTH = 2
GRID_W = 64
HEAD_DIM = 64
ATTN_HEADS = 8
KV_HEADS = 2
ATTN_WIDTH = 512
KV_WIDTH = 128
BLOCK = 128
ROPE_BASE = 10000.0
RET_HEADS = 4
RET_WIDTH = 512
RET_V_DIM = 128
RET_QK_DIM = 64
RET_QK_WIDTH = 256
IN_WIDTH = 2304
D_FF = 2816
N_MOD = 9
RMS_EPS = 1e-6
GN_EPS = 1e-5
NEG_INF = -1e30
LANES = 128
FF_CHUNK = 256
VMEM_LIMIT = 56 * 1024 * 1024

O_QA, O_KA, O_VA, O_QR, O_KR, O_VR, O_GR = 0, 512, 640, 768, 1024, 1280, 1792


def _params(*sem):
    return pltpu.CompilerParams(dimension_semantics=sem, vmem_limit_bytes=VMEM_LIMIT)


def _rms(x, g):
    ms = jnp.mean(x * x, axis=-1, keepdims=True)
    return x * lax.rsqrt(ms + RMS_EPS) * g


def _silu(x):
    return x * jax.nn.sigmoid(x)


def _dot(a, b):
    return jnp.dot(a, b, preferred_element_type=F32)


def _mod_kernel(c_ref, w_ref, b_ref, o_ref):
    a = _silu(c_ref[...]).astype(BF16)
    o_ref[0] = _dot(a, w_ref[0].astype(BF16)) + b_ref[0]


def _modulation(cs, ada_w, ada_b):
    rows = cs.shape[0]
    tn = 1024
    return pl.pallas_call(
        _mod_kernel,
        grid=(DEPTH, N_MOD * D_MODEL // tn),
        in_specs=[
            pl.BlockSpec((rows, D_MODEL), lambda l, j: (0, 0)),
            pl.BlockSpec((1, D_MODEL, tn), lambda l, j: (l, 0, j)),
            pl.BlockSpec((1, 1, tn), lambda l, j: (l, 0, j)),
        ],
        out_specs=pl.BlockSpec((1, rows, tn), lambda l, j: (l, 0, j)),
        out_shape=jax.ShapeDtypeStruct((DEPTH, rows, N_MOD * D_MODEL), F32),
        compiler_params=_params("arbitrary", "arbitrary"),
        name="adaln_mod",
    )(cs, ada_w, ada_b.reshape(DEPTH, 1, N_MOD * D_MODEL))


def _ffn_kernel(h_ref, mod_ref, gpre_ref, gpost_ref, wi_ref, wo_ref, o_ref):
    x = h_ref[0]
    mod = mod_ref[0]
    u = (_rms(x, gpre_ref[...]) * (1.0 + mod[1:2]) + mod[0:1]).astype(BF16)
    acc = jnp.zeros(x.shape, F32)
    for c in range(D_FF // FF_CHUNK):
        lo = c * FF_CHUNK
        a = _dot(u, wi_ref[:, lo:lo + FF_CHUNK])
        b = _dot(u, wi_ref[:, D_FF + lo:D_FF + lo + FF_CHUNK])
        g = (_silu(a) * b).astype(BF16)
        acc = acc + _dot(g, wo_ref[lo:lo + FF_CHUNK, :])
    o_ref[0] = x + 0.5 * mod[2:3] * _rms(acc, gpost_ref[...])


def _ffn(h, mod3, g_pre, g_post, wi, wo, tm):
    b, t, d = h.shape
    return pl.pallas_call(
        _ffn_kernel,
        grid=(b, t // tm),
        in_specs=[
            pl.BlockSpec((1, tm, d), lambda i, j: (i, j, 0)),
            pl.BlockSpec((1, 3, d), lambda i, j: (i, 0, 0)),
            pl.BlockSpec((1, d), lambda i, j: (0, 0)),
            pl.BlockSpec((1, d), lambda i, j: (0, 0)),
            pl.BlockSpec((d, 2 * D_FF), lambda i, j: (0, 0), pipeline_mode=pl.Buffered(1)),
            pl.BlockSpec((D_FF, d), lambda i, j: (0, 0), pipeline_mode=pl.Buffered(1)),
        ],
        out_specs=pl.BlockSpec((1, tm, d), lambda i, j: (i, j, 0)),
        out_shape=jax.ShapeDtypeStruct(h.shape, F32),
        compiler_params=_params("parallel", "parallel"),
        name="ffn",
    )(h, mod3, g_pre, g_post, wi, wo)


def _rope(x, cos, s_hi, s_lo, half):
    out = []
    for j in range(x.shape[1] // LANES):
        xs = x[:, j * LANES:(j + 1) * LANES]
        up = pltpu.roll(xs, LANES - half, axis=1)
        dn = pltpu.roll(xs, half, axis=1)
        out.append(xs * cos + up * s_hi + dn * s_lo)
    return out


def _proj_kernel(h_ref, mod_ref, gpre_ref, w_ref, ca_ref, sah_ref, sal_ref, cr_ref, srh_ref, srl_ref,
                 qa_ref, kat_ref, va_ref, vas_ref, qr_ref, krt_ref, vr_ref, gr_ref):
    x = h_ref[0]
    mod = mod_ref[0]
    u = (_rms(x, gpre_ref[...]) * (1.0 + mod[1:2]) + mod[0:1]).astype(BF16)
    p = _dot(u, w_ref[...])
    ca, sah, sal = ca_ref[...], sah_ref[...], sal_ref[...]
    cr, srh, srl = cr_ref[...], srh_ref[...], srl_ref[...]
    qk_scale = HEAD_DIM ** -0.5

    qa = _rope(p[:, O_QA:O_QA + ATTN_WIDTH], ca, sah, sal, HEAD_DIM // 4)
    for j, t in enumerate(qa):
        qa_ref[0, :, j * LANES:(j + 1) * LANES] = (t * qk_scale).astype(BF16)
    (ka,) = _rope(p[:, O_KA:O_KA + KV_WIDTH], ca, sah, sal, HEAD_DIM // 4)
    kat_ref[0] = ka.T.astype(BF16)
    va = p[:, O_VA:O_VA + KV_WIDTH]
    va_ref[0] = va.astype(BF16)
    vas_ref[0] = pltpu.roll(va, HEAD_DIM, axis=1).astype(BF16)

    qr = _rope(p[:, O_QR:O_QR + RET_QK_WIDTH], cr, srh, srl, RET_QK_DIM // 2)
    for j, t in enumerate(qr):
        qr_ref[0, :, j * LANES:(j + 1) * LANES] = t.astype(BF16)
    kr = _rope(p[:, O_KR:O_KR + RET_QK_WIDTH] * (RET_QK_DIM ** -0.5), cr, srh, srl, RET_QK_DIM // 2)
    for j, t in enumerate(kr):
        krt_ref[0, j * LANES:(j + 1) * LANES, :] = t.T.astype(BF16)
    vr_ref[0] = p[:, O_VR:O_VR + RET_WIDTH].astype(BF16)
    gr_ref[0] = p[:, O_GR:O_GR + RET_WIDTH].astype(BF16)


def _project(h, mod3, g_pre, w_in, tabs, tm):
    b, t, d = h.shape
    row = lambda w: pl.BlockSpec((1, tm, w), lambda j, i: (i, j, 0))
    col = lambda w: pl.BlockSpec((1, w, tm), lambda j, i: (i, 0, j))
    tab = pl.BlockSpec((tm, LANES), lambda j, i: (j, 0))
    sds = lambda *s: jax.ShapeDtypeStruct(s, BF16)
    return pl.pallas_call(
        _proj_kernel,
        grid=(t // tm, b),
        in_specs=[
            pl.BlockSpec((1, tm, d), lambda j, i: (i, j, 0)),
            pl.BlockSpec((1, 3, d), lambda j, i: (i, 0, 0)),
            pl.BlockSpec((1, d), lambda j, i: (0, 0)),
            pl.BlockSpec((d, IN_WIDTH), lambda j, i: (0, 0), pipeline_mode=pl.Buffered(1)),
            tab, tab, tab, tab, tab, tab,
        ],
        out_specs=[row(ATTN_WIDTH), col(KV_WIDTH), row(KV_WIDTH), row(KV_WIDTH),
                   row(RET_QK_WIDTH), col(RET_QK_WIDTH), row(RET_WIDTH), row(RET_WIDTH)],
        out_shape=[sds(b, t, ATTN_WIDTH), sds(b, KV_WIDTH, t), sds(b, t, KV_WIDTH), sds(b, t, KV_WIDTH),
                   sds(b, t, RET_QK_WIDTH), sds(b, RET_QK_WIDTH, t), sds(b, t, RET_WIDTH), sds(b, t, RET_WIDTH)],
        compiler_params=_params("parallel", "parallel"),
        name="mixer_proj",
    )(h, mod3, g_pre, w_in, *tabs)


def _attn_kernel(sink_ref, q_ref, *refs, band):
    o_ref = refs[-1]
    if band:
        kl, kc, kr, vl, vc, vr, sl, sc, sr, kx, vx, sx = refs[:-1]
        n, nb = pl.program_id(1), pl.num_programs(1)
        kt = jnp.concatenate([kl[0], kc[0], kr[0], kx[0]], axis=1)
        v = jnp.concatenate([vl[0], vc[0], vr[0], vx[0]], axis=0)
        vs = jnp.concatenate([sl[0], sc[0], sr[0], sx[0]], axis=0)
    else:
        kx, vx, sx = refs[:-1]
        kt, v, vs = kx[0], vx[0], sx[0]
    nk = kt.shape[1]
    tq = q_ref.shape[1]
    rows = 2 * tq

    if band:
        qi = lax.broadcasted_iota(jnp.int32, (rows, 3 * BLOCK), 0) % tq
        kj = lax.broadcasted_iota(jnp.int32, (rows, 3 * BLOCK), 1)
        first_key = jnp.where(n > 0, 0, BLOCK)
        last_key = jnp.where(n < nb - 1, 3 * BLOCK - 1, 2 * BLOCK - 1)
        valid = (kj >= jnp.maximum(qi, first_key)) & (kj <= jnp.minimum(qi + 2 * BLOCK, last_key))

    krow = lax.broadcasted_iota(jnp.int32, kt.shape, 0)
    vlane = lax.broadcasted_iota(jnp.int32, v.shape, 1)
    zero_k = jnp.zeros_like(kt)
    zero_v = jnp.zeros_like(v)
    ones_e = jnp.where(vlane == 0, 1.0, 0.0).astype(BF16)
    ones_o = jnp.where(vlane == 1, 1.0, 0.0).astype(BF16)
    top_rows = lax.broadcasted_iota(jnp.int32, (rows, 1), 0) < tq
    olane = lax.broadcasted_iota(jnp.int32, (rows, LANES), 1)

    for kh in range(KV_HEADS):
        if kh == 0:
            k_even = jnp.where(krow < HEAD_DIM, kt, zero_k)
            k_odd = jnp.concatenate([zero_k[:HEAD_DIM], kt[:HEAD_DIM]], axis=0)
            v_even = jnp.where(vlane < HEAD_DIM, v, zero_v)
            v_odd = jnp.where(vlane >= HEAD_DIM, vs, zero_v)
        else:
            k_even = jnp.concatenate([kt[HEAD_DIM:], zero_k[:HEAD_DIM]], axis=0)
            k_odd = jnp.where(krow >= HEAD_DIM, kt, zero_k)
            v_even = jnp.where(vlane < HEAD_DIM, vs, zero_v)
            v_odd = jnp.where(vlane >= HEAD_DIM, v, zero_v)
        vblk = jnp.concatenate([jnp.concatenate([v_even, ones_e], axis=1),
                                jnp.concatenate([v_odd, ones_o], axis=1)], axis=0)
        q2 = jnp.concatenate([q_ref[0, :, (2 * kh) * LANES:(2 * kh + 1) * LANES],
                              q_ref[0, :, (2 * kh + 1) * LANES:(2 * kh + 2) * LANES]], axis=0)
        es, sinks = [], []
        for par, k_sel in enumerate((k_even, k_odd)):
            s = _dot(q2, k_sel)
            if band:
                s = jnp.concatenate([jnp.where(valid, s[:, :3 * BLOCK], NEG_INF), s[:, 3 * BLOCK:]], axis=1)
            sk = jnp.where(top_rows, sink_ref[4 * kh + par], sink_ref[4 * kh + 2 + par])
            m = jnp.maximum(jnp.max(s, axis=1, keepdims=True), sk)
            es.append(jnp.exp(s - m).astype(BF16))
            sinks.append(jnp.exp(sk - m))
        res = _dot(jnp.concatenate(es, axis=1), vblk)
        den_e = res[:, LANES:LANES + 1] + sinks[0]
        den_o = res[:, LANES + 1:LANES + 2] + sinks[1]
        out = (res[:, :LANES] * jnp.where(olane < HEAD_DIM, 1.0 / den_e, 1.0 / den_o)).astype(BF16)
        o_ref[0, :, (2 * kh) * LANES:(2 * kh + 1) * LANES] = out[:tq]
        o_ref[0, :, (2 * kh + 1) * LANES:(2 * kh + 2) * LANES] = out[tq:]


def _attention(sink, q, kat, va, vas, kxt, vx, vxs, band):
    b, t, _ = q.shape
    lctx = kxt.shape[2]
    smem = pl.BlockSpec(memory_space=pltpu.SMEM)
    if band:
        tq = BLOCK
        nb = t // BLOCK
        lo = lambda n: jnp.maximum(n - 1, 0)
        hi = lambda n: jnp.minimum(n + 1, nb - 1)
        kspec = lambda f: pl.BlockSpec((1, KV_WIDTH, BLOCK), lambda i, n: (i, 0, f(n)))
        vspec = lambda f: pl.BlockSpec((1, BLOCK, KV_WIDTH), lambda i, n: (i, f(n), 0))
        same = lambda n: n
        specs = [kspec(lo), kspec(same), kspec(hi)] + [vspec(lo), vspec(same), vspec(hi)] * 2
        args = [kat, kat, kat, va, va, va, vas, vas, vas]
    else:
        tq = t
        specs, args = [], []
    ctx_specs = [pl.BlockSpec((1, KV_WIDTH, lctx), lambda i, n: (i, 0, 0)),
                 pl.BlockSpec((1, lctx, KV_WIDTH), lambda i, n: (i, 0, 0)),
                 pl.BlockSpec((1, lctx, KV_WIDTH), lambda i, n: (i, 0, 0))]
    return pl.pallas_call(
        functools.partial(_attn_kernel, band=band),
        grid=(b, t // tq),
        in_specs=[smem, pl.BlockSpec((1, tq, ATTN_WIDTH), lambda i, n: (i, n, 0))] + specs + ctx_specs,
        out_specs=pl.BlockSpec((1, tq, ATTN_WIDTH), lambda i, n: (i, n, 0)),
        out_shape=jax.ShapeDtypeStruct((b, t, ATTN_WIDTH), BF16),
        compiler_params=_params("parallel", "parallel"),
        name="band_attention" if band else "context_attention",
    )(sink, q, *args, kxt, vx, vxs)


def _ret_state_kernel(lg_ref, kt_ref, v_ref, s0f_ref, s0b_ref, sf_ref, sb_ref, ff_ref, fb_ref, st_ref):
    nc = kt_ref.shape[2] // BLOCK
    st_ref[0] = s0f_ref[0]
    st_ref[1] = s0b_ref[0]
    pos = lax.broadcasted_iota(jnp.int32, (BLOCK, 1), 0).astype(F32)

    def body(i, carry):
        for direction in range(2):
            c = i if direction == 0 else nc - 1 - i
            start = pl.multiple_of(c * BLOCK, BLOCK)
            out_ref = sf_ref if direction == 0 else sb_ref
            for hh in range(RET_HEADS):
                lg = lg_ref[direction * RET_HEADS + hh]
                zeta = jnp.exp((pos if direction == 1 else (BLOCK - 1.0) - pos) * lg)
                kt = kt_ref[0, hh * RET_QK_DIM:(hh + 1) * RET_QK_DIM, pl.ds(start, BLOCK)]
                vv = v_ref[0, pl.ds(start, BLOCK), hh * RET_V_DIM:(hh + 1) * RET_V_DIM]
                inc = _dot(kt, (vv.astype(F32) * zeta).astype(BF16))
                state = st_ref[direction, hh]
                out_ref[0, c, hh] = state.astype(BF16)
                st_ref[direction, hh] = jnp.exp(jnp.full((1, 1), BLOCK * 1.0, F32) * lg) * state + inc
        return carry

    lax.fori_loop(0, nc, body, 0)
    ff_ref[0] = st_ref[0]
    fb_ref[0] = st_ref[1]


def _ret_states(lg, krt, vr, s0f, s0b):
    b, _, t = krt.shape
    nc = t // BLOCK
    st = (RET_HEADS, RET_QK_DIM, RET_V_DIM)
    return pl.pallas_call(
        _ret_state_kernel,
        grid=(b,),
        in_specs=[
            pl.BlockSpec(memory_space=pltpu.SMEM),
            pl.BlockSpec((1, RET_QK_WIDTH, t), lambda i: (i, 0, 0)),
            pl.BlockSpec((1, t, RET_WIDTH), lambda i: (i, 0, 0)),
            pl.BlockSpec((1,) + st, lambda i: (i, 0, 0, 0)),
            pl.BlockSpec((1,) + st, lambda i: (i, 0, 0, 0)),
        ],
        out_specs=[
            pl.BlockSpec((1, nc) + st, lambda i: (i, 0, 0, 0, 0)),
            pl.BlockSpec((1, nc) + st, lambda i: (i, 0, 0, 0, 0)),
            pl.BlockSpec((1,) + st, lambda i: (i, 0, 0, 0)),
            pl.BlockSpec((1,) + st, lambda i: (i, 0, 0, 0)),
        ],
        out_shape=[
            jax.ShapeDtypeStruct((b, nc) + st, BF16),
            jax.ShapeDtypeStruct((b, nc) + st, BF16),
            jax.ShapeDtypeStruct((b,) + st, F32),
            jax.ShapeDtypeStruct((b,) + st, F32),
        ],
        scratch_shapes=[pltpu.VMEM((2,) + st, F32)],
        compiler_params=_params("parallel"),
        name="retention_states",
    )(lg, krt, vr, s0f, s0b)


def _ret_out_kernel(lg_ref, q_ref, kt_ref, v_ref, g_ref, sf_ref, sb_ref, gn_ref, o_ref):
    ii = lax.broadcasted_iota(jnp.int32, (BLOCK, BLOCK), 0)
    jj = lax.broadcasted_iota(jnp.int32, (BLOCK, BLOCK), 1)
    diff = (ii - jj).astype(F32)
    pos = lax.broadcasted_iota(jnp.int32, (BLOCK, 1), 0).astype(F32)
    krow = lax.broadcasted_iota(jnp.int32, (LANES, BLOCK), 0)
    zero_s = jnp.zeros((RET_QK_DIM, RET_V_DIM), BF16)
    for hh in range(RET_HEADS):
        lgf = lg_ref[hh]
        lgb = lg_ref[RET_HEADS + hh]
        pair, odd = hh // 2, hh % 2
        q2 = q_ref[0, :, pair * LANES:(pair + 1) * LANES]
        kt2 = kt_ref[0, pair * LANES:(pair + 1) * LANES, :]
        kt = jnp.where((krow >= RET_QK_DIM) if odd else (krow < RET_QK_DIM), kt2, jnp.zeros_like(kt2))
        decay = jnp.where(diff >= 0, jnp.exp(jnp.maximum(diff, 0.0) * lgf),
                          jnp.exp(jnp.maximum(-diff, 0.0) * lgb))
        scores = (_dot(q2, kt) * decay).astype(BF16)
        y = _dot(scores, v_ref[0, :, hh * RET_V_DIM:(hh + 1) * RET_V_DIM])
        q2f = q2.astype(F32)
        xi_f = jnp.exp((pos + 1.0) * lgf)
        xi_b = jnp.exp((BLOCK * 1.0 - pos) * lgb)
        qx = jnp.concatenate([(q2f * xi_f).astype(BF16), (q2f * xi_b).astype(BF16)], axis=1)
        sf, sb = sf_ref[0, 0, hh], sb_ref[0, 0, hh]
        sx = jnp.concatenate([zero_s, sf, zero_s, sb] if odd else [sf, zero_s, sb, zero_s], axis=0)
        y = y + _dot(qx, sx)
        mu = jnp.mean(y, axis=-1, keepdims=True)
        yc = y - mu
        var = jnp.mean(yc * yc, axis=-1, keepdims=True)
        yn = yc * lax.rsqrt(var + GN_EPS) * gn_ref[:, hh * RET_V_DIM:(hh + 1) * RET_V_DIM]
        g = g_ref[0, :, hh * RET_V_DIM:(hh + 1) * RET_V_DIM].astype(F32)
        o_ref[0, :, hh * RET_V_DIM:(hh + 1) * RET_V_DIM] = (_silu(g) * yn).astype(BF16)


def _ret_outputs(lg, qr, krt, vr, gr, sf, sb, gn):
    b, t, _ = qr.shape
    st = (RET_HEADS, RET_QK_DIM, RET_V_DIM)
    return pl.pallas_call(
        _ret_out_kernel,
        grid=(b, t // BLOCK),
        in_specs=[
            pl.BlockSpec(memory_space=pltpu.SMEM),
            pl.BlockSpec((1, BLOCK, RET_QK_WIDTH), lambda i, n: (i, n, 0)),
            pl.BlockSpec((1, RET_QK_WIDTH, BLOCK), lambda i, n: (i, 0, n)),
            pl.BlockSpec((1, BLOCK, RET_WIDTH), lambda i, n: (i, n, 0)),
            pl.BlockSpec((1, BLOCK, RET_WIDTH), lambda i, n: (i, n, 0)),
            pl.BlockSpec((1, 1) + st, lambda i, n: (i, n, 0, 0, 0)),
            pl.BlockSpec((1, 1) + st, lambda i, n: (i, n, 0, 0, 0)),
            pl.BlockSpec((1, RET_WIDTH), lambda i, n: (0, 0)),
        ],
        out_specs=pl.BlockSpec((1, BLOCK, RET_WIDTH), lambda i, n: (i, n, 0)),
        out_shape=jax.ShapeDtypeStruct((b, t, RET_WIDTH), BF16),
        compiler_params=_params("parallel", "parallel"),
        name="retention_out",
    )(lg, qr, krt, vr, gr, sf, sb, gn)


def _mixout_kernel(h_ref, a_ref, r_ref, mod_ref, gpost_ref, w_ref, o_ref):
    y = _dot(a_ref[0], w_ref[:ATTN_WIDTH, :]) + _dot(r_ref[0], w_ref[ATTN_WIDTH:, :])
    o_ref[0] = h_ref[0] + mod_ref[0][2:3] * _rms(y, gpost_ref[...])


def _mixer_out(h, attn, ret, mod3, g_post, w_out, tm):
    b, t, d = h.shape
    return pl.pallas_call(
        _mixout_kernel,
        grid=(b, t // tm),
        in_specs=[
            pl.BlockSpec((1, tm, d), lambda i, j: (i, j, 0)),
            pl.BlockSpec((1, tm, ATTN_WIDTH), lambda i, j: (i, j, 0)),
            pl.BlockSpec((1, tm, RET_WIDTH), lambda i, j: (i, j, 0)),
            pl.BlockSpec((1, 3, d), lambda i, j: (i, 0, 0)),
            pl.BlockSpec((1, d), lambda i, j: (0, 0)),
            pl.BlockSpec((d, d), lambda i, j: (0, 0), pipeline_mode=pl.Buffered(1)),
        ],
        out_specs=pl.BlockSpec((1, tm, d), lambda i, j: (i, j, 0)),
        out_shape=jax.ShapeDtypeStruct(h.shape, F32),
        compiler_params=_params("parallel", "parallel"),
        name="mixer_out",
    )(h, attn, ret, mod3, g_post, w_out)


def _rope_tables(n_tokens):
    rows = n_tokens // GRID_W
    row = jnp.repeat(jnp.arange(rows, dtype=F32), GRID_W)
    col = jnp.tile(jnp.arange(GRID_W, dtype=F32), rows)
    nf = HEAD_DIM // 4
    inv = ROPE_BASE ** (-jnp.arange(nf, dtype=F32) / nf)
    ar, ac = row[:, None] * inv, col[:, None] * inv
    z = jnp.zeros_like(ar)
    rep = lambda parts, k: jnp.tile(jnp.concatenate(parts, axis=-1), (1, k))
    attn = (rep([jnp.cos(ar), jnp.cos(ar), jnp.cos(ac), jnp.cos(ac)], 2),
            rep([-jnp.sin(ar), z, -jnp.sin(ac), z], 2),
            rep([z, jnp.sin(ar), z, jnp.sin(ac)], 2))
    t = jnp.arange(n_tokens, dtype=F32)
    inv_r = ROPE_BASE ** (-jnp.linspace(0.0, 1.0, RET_QK_DIM // 2, dtype=F32))
    ang = t[:, None] * inv_r
    zr = jnp.zeros_like(ang)
    ret = (rep([jnp.cos(ang), jnp.cos(ang)], 2), rep([-jnp.sin(ang), zr], 2), rep([zr, jnp.sin(ang)], 2))
    return attn + ret


def _identity_tables(n_tokens):
    one = jnp.ones((n_tokens, LANES), F32)
    zero = jnp.zeros((n_tokens, LANES), F32)
    return (one, zero, zero, one, zero, zero)


def kernel(x, c, ctx, c_ctx, ada_w, ada_b, norm_pre, norm_post, ffn1_wi, ffn1_wo, ffn2_wi, ffn2_wo,
           w_in, w_out, attn_sink, ret_decay_fwd, ret_decay_bwd, ret_gn):
    b, s, d = x.shape
    lctx = ctx.shape[1]
    tm = 512
    tabs = _rope_tables(s)
    tabs_c = _identity_tables(lctx)

    cs = jnp.concatenate([c, c_ctx[None], jnp.zeros((8 - b - 1, d), F32)], axis=0)
    mod_all = _modulation(cs, ada_w, ada_b).reshape(DEPTH, 8, N_MOD, d)

    wi1, wo1 = ffn1_wi.astype(BF16), ffn1_wo.astype(BF16)
    wi2, wo2 = ffn2_wi.astype(BF16), ffn2_wo.astype(BF16)
    w_in_b, w_out_b = w_in.astype(BF16), w_out.astype(BF16)
    zero_state = jnp.zeros((b, RET_HEADS, RET_QK_DIM, RET_V_DIM), F32)

    h, hc = x, ctx
    for l in range(DEPTH):
        last = l == DEPTH - 1
        mod = mod_all[l, :b]
        mod_c = jnp.broadcast_to(mod_all[l, b][None], (b, N_MOD, d))
        gp = lambda k: norm_pre[l, k][None]
        gq = lambda k: norm_post[l, k][None]
        lg = jnp.concatenate([jax.nn.log_sigmoid(ret_decay_fwd[l].astype(F32)),
                              jax.nn.log_sigmoid(ret_decay_bwd[l].astype(F32))])
        sink = attn_sink[l].astype(F32)
        gn = ret_gn[l][None].astype(F32)

        h = _ffn(h, mod[:, 0:3], gp(0), gq(0), wi1[l], wo1[l], tm)
        hc = _ffn(hc, mod_c[:, 0:3], gp(0), gq(0), wi1[l], wo1[l], lctx)

        qa, kat, va, vas, qr, krt, vr, gr = _project(h, mod[:, 3:6], gp(1), w_in_b[l], tabs, tm)
        qac, katc, vac, vasc, qrc, krtc, vrc, grc = _project(hc, mod_c[:, 3:6], gp(1), w_in_b[l], tabs_c, lctx)

        sfc, sbc, s_f, s_b = _ret_states(lg, krtc, vrc, zero_state, zero_state)
        sf, sb, _, _ = _ret_states(lg, krt, vr, s_f, s_b)

        attn = _attention(sink, qa, kat, va, vas, katc, vac, vasc, band=True)
        ret = _ret_outputs(lg, qr, krt, vr, gr, sf, sb, gn)
        h = _mixer_out(h, attn, ret, mod[:, 3:6], gq(1), w_out_b[l], tm)

        if not last:
            attn_c = _attention(sink, qac, None, None, None, katc, vac, vasc, band=False)
            ret_c = _ret_outputs(lg, qrc, krtc, vrc, grc, sfc, sbc, gn)
            hc = _mixer_out(hc, attn_c, ret_c, mod_c[:, 3:6], gq(1), w_out_b[l], lctx)
            hc = _ffn(hc, mod_c[:, 6:9], gp(2), gq(2), wi2[l], wo2[l], lctx)

        h = _ffn(h, mod[:, 6:9], gp(2), gq(2), wi2[l], wo2[l], tm)
    return h
```
